```python
import math
import jax, jax.numpy as jnp
from jax import lax
import numpy as np

D_MODEL = 2048
BATCH = 4
SEQ = 2048
DEPTH = 4
DEC_BATCH = 128
DEC_SEQ = 1
PAST_LEN = 8192
PAGE_SIZE = 128

N_A = DEPTH // 2
N_B = DEPTH - N_A
D_RNN = D_MODEL
N_RNN_HEADS = 8
RNN_BLOCK = D_RNN // N_RNN_HEADS
RG_CONV = 4
RG_C = 8.0
N_HEADS = 16
Q_LORA = 512
KV_LORA = 512
NOPE_DIM = 128
ROPE_DIM = 64
V_DIM = 128
ROPE_THETA = 10000.0
ATTN_SCALE = 1.0 / math.sqrt(NOPE_DIM + ROPE_DIM)
Q_BLOCK = 128
D_FF = 3 * D_MODEL
FFN_CONV = 3
EPS = 1e-6

kernel_name = "yoco_rglru_mla_convffn_step"


def rms_norm(x, g):
    xf = x.astype(jnp.float32)
    y = xf * lax.rsqrt(jnp.mean(xf * xf, axis=-1, keepdims=True) + EPS)
    return (y * g.astype(jnp.float32)).astype(x.dtype)


def rope(x, pos):
    half = ROPE_DIM // 2
    inv = ROPE_THETA ** (-jnp.arange(half, dtype=jnp.float32) * 2.0 / ROPE_DIM)
    ang = pos[:, None] * inv[None, :]
    shape = (1, ang.shape[0]) + (1,) * (x.ndim - 3) + (half,)
    cos = jnp.cos(ang).reshape(shape)
    sin = jnp.sin(ang).reshape(shape)
    xf = x.astype(jnp.float32)
    x1, x2 = xf[..., :half], xf[..., half:]
    return jnp.concatenate([x1 * cos - x2 * sin, x2 * cos + x1 * sin], axis=-1).astype(x.dtype)


def causal_dwconv(x, buf, w, b):
    k_w = w.shape[0]
    t = x.shape[1]
    xp = jnp.concatenate([buf.astype(x.dtype), x], axis=1)
    y = b + xp[:, 0:t] * w[0]
    for k in range(1, k_w):
        y = y + xp[:, k:k + t] * w[k]
    return y, xp[:, t:]


def block_diag(x, w):
    xh = x.reshape(x.shape[:-1] + (N_RNN_HEADS, RNN_BLOCK))
    return jnp.einsum('...hi,hij->...hj', xh, w).reshape(x.shape)


def rglru(x, h0, w_a, b_a, w_i, b_i, lam):
    xf = x.astype(jnp.float32)
    r = jax.nn.sigmoid(block_diag(xf, w_a.astype(jnp.float32)) + b_a)
    i = jax.nn.sigmoid(block_diag(xf, w_i.astype(jnp.float32)) + b_i)
    log_a = -RG_C * r * jax.nn.softplus(-lam.astype(jnp.float32))
    a = jnp.exp(log_a)
    u = jnp.sqrt(-jnp.expm1(2.0 * log_a)) * (i * xf)

    def step(h, au):
        a_t, u_t = au
        h = a_t * h + u_t
        return h, h

    h_last, hs = lax.scan(step, h0.astype(jnp.float32), (a.swapaxes(0, 1), u.swapaxes(0, 1)))
    return hs.swapaxes(0, 1).astype(x.dtype), h_last


def recurrent_block(xn, h0, buf0, w_in, conv_w, conv_b, w_a, b_a, w_i, b_i, lam, w_o):
    proj = xn @ w_in
    xb, yb = proj[..., :D_RNN], proj[..., D_RNN:]
    xc, new_buf = causal_dwconv(xb, buf0, conv_w, conv_b)
    hs, h_last = rglru(xc, h0, w_a, b_a, w_i, b_i, lam)
    return (jax.nn.gelu(yb, approximate=True) * hs) @ w_o, h_last, new_buf


def conv_ffn(xn, buf0, w_g, w_u, cw, cb, w_d):
    gc, new_buf = causal_dwconv(xn @ w_g, buf0, cw, cb)
    h = jax.nn.gelu(gc, approximate=True) * (xn @ w_u)
    return h @ w_d, new_buf


def shared_kv(x, pos, kv_in_g, w_dkv, kv_norm_g):
    kv = rms_norm(x, kv_in_g) @ w_dkv
    c = rms_norm(kv[..., :KV_LORA], kv_norm_g)
    kr = rope(kv[..., KV_LORA:], pos)
    return c, kr


def mla_query(xn, pos, w_dq, q_g, w_uq, w_uk):
    cq = rms_norm(xn @ w_dq, q_g)
    q = jnp.einsum('bsl,lhd->bshd', cq, w_uq)
    q_nope, q_rope = q[..., :NOPE_DIM], rope(q[..., NOPE_DIM:], pos)
    q_lat = jnp.einsum('bshd,chd->bshc', q_nope, w_uk)
    return q_lat, q_rope


def prompt_attention(q_lat, q_rope, c, kr):
    b, s, h, cd = q_lat.shape
    nb = s // Q_BLOCK
    qlb = q_lat.reshape(b, nb, Q_BLOCK, h, cd).swapaxes(0, 1)
    qrb = q_rope.reshape(b, nb, Q_BLOCK, h, ROPE_DIM).swapaxes(0, 1)
    k_pos = jnp.arange(s)

    def blk(args):
        ql, qr, bi = args
        sc = (jnp.einsum('bqhc,bkc->bhqk', ql, c).astype(jnp.float32)
              + jnp.einsum('bqhr,bkr->bhqk', qr, kr).astype(jnp.float32)) * ATTN_SCALE
        q_pos = bi * Q_BLOCK + jnp.arange(Q_BLOCK)
        sc = jnp.where(k_pos[None, :] <= q_pos[:, None], sc, -jnp.inf)
        p = jax.nn.softmax(sc, axis=-1).astype(c.dtype)
        return jnp.einsum('bhqk,bkc->bqhc', p, c)

    o = lax.map(blk, (qlb, qrb, jnp.arange(nb)))
    return o.swapaxes(0, 1).reshape(b, s, h, cd)


def sample_attention(q_lat, q_rope, c_new, kr_new, c_past, kr_past):
    t = q_lat.shape[1]
    past = c_past.shape[1]
    s_past = (jnp.einsum('bqhc,bkc->bhqk', q_lat, c_past).astype(jnp.float32)
              + jnp.einsum('bqhr,bkr->bhqk', q_rope, kr_past).astype(jnp.float32)) * ATTN_SCALE
    s_new = (jnp.einsum('bqhc,bkc->bhqk', q_lat, c_new).astype(jnp.float32)
             + jnp.einsum('bqhr,bkr->bhqk', q_rope, kr_new).astype(jnp.float32)) * ATTN_SCALE
    s_new = jnp.where(jnp.tril(jnp.ones((t, t), dtype=bool)), s_new, -jnp.inf)
    p = jax.nn.softmax(jnp.concatenate([s_past, s_new], axis=-1), axis=-1).astype(c_new.dtype)
    return (jnp.einsum('bhqk,bkc->bqhc', p[..., :past], c_past)
            + jnp.einsum('bhqk,bkc->bqhc', p[..., past:], c_new))


def trunk(x, pos, rg_h0, rg_buf0, ffn_buf0, attend, P):
    new_h, new_rg_buf, new_ffn_buf = [], [], []
    c = kr = None
    bsz, t = x.shape[0], x.shape[1]
    for l in range(DEPTH):
        g = P['norm_g'][l]
        xn = rms_norm(x, g[0])
        if l < N_A:
            out, h_last, nb = recurrent_block(
                xn, rg_h0[l], rg_buf0[l], P['w_rg_in'][l], P['conv_rg_w'][l], P['conv_rg_b'][l],
                P['w_rg_a'][l], P['b_rg_a'][l], P['w_rg_i'][l], P['b_rg_i'][l],
                P['rg_lambda'][l], P['w_rg_o'][l])
            new_h.append(h_last)
            new_rg_buf.append(nb)
        else:
            j = l - N_A
            q_lat, q_rope = mla_query(xn, pos, P['w_dq'][j], P['q_norm_g'][j], P['w_uq'][j], P['w_uk'])
            o_lat = attend(q_lat, q_rope, c, kr)
            o = jnp.einsum('bshc,chv->bshv', o_lat, P['w_uv']).reshape(bsz, t, N_HEADS * V_DIM)
            out = o @ P['w_mla_o'][j]
        x = x + rms_norm(out, g[1])
        f, fb = conv_ffn(rms_norm(x, g[2]), ffn_buf0[l], P['w_ffn_g'][l], P['w_ffn_u'][l],
                         P['ffn_conv_w'][l], P['ffn_conv_b'][l], P['w_ffn_d'][l])
        new_ffn_buf.append(fb)
        x = x + rms_norm(f, g[3])
        if l == N_A - 1:
            c, kr = shared_kv(x, pos, P['kv_in_g'], P['w_dkv'], P['kv_norm_g'])
    return x, jnp.stack(new_h), jnp.stack(new_rg_buf), jnp.stack(new_ffn_buf), c, kr


def setup_inputs(seed: int = 0) -> dict:
    key = jax.random.key(seed)
    ks = jax.random.split(key, 40)
    f32 = jnp.float32
    n_pages = PAST_LEN // PAGE_SIZE
    n_phys = (DEC_BATCH * n_pages * 5) // 4

    def nrm(k, shape, fan_in):
        return jax.random.normal(k, shape, f32) * (fan_in ** -0.5)

    a0 = jax.random.uniform(ks[20], (N_A, D_RNN), f32, 0.9, 0.999)
    a_base = a0 ** (1.0 / RG_C)
    rg_lambda = jnp.log(a_base) - jnp.log1p(-a_base)
    page_table = jax.random.permutation(ks[7], n_phys)[:DEC_BATCH * n_pages].reshape(
        DEC_BATCH, n_pages).astype(jnp.int32)
    return {
        'x_prompt': jax.random.normal(ks[0], (BATCH, SEQ, D_MODEL), f32),
        'x_sample': jax.random.normal(ks[1], (DEC_BATCH, DEC_SEQ, D_MODEL), f32),
        'state_rglru_h': 0.5 * jax.random.normal(ks[2], (N_A, DEC_BATCH, D_RNN), f32),
        'state_rglru_conv': jax.random.normal(ks[3], (N_A, DEC_BATCH, RG_CONV - 1, D_RNN), f32),
        'state_ffn_conv': jax.random.normal(ks[4], (DEPTH, DEC_BATCH, FFN_CONV - 1, D_FF), f32),
        'cache_kv_latent': jax.random.normal(ks[5], (n_phys, PAGE_SIZE, KV_LORA), f32),
        'cache_k_rope': jax.random.normal(ks[6], (n_phys, PAGE_SIZE, ROPE_DIM), f32),
        'page_table': page_table,
        'norm_g': 1.0 + 0.05 * jax.random.normal(ks[8], (DEPTH, 4, D_MODEL), f32),
        'w_rg_in': nrm(ks[9], (N_A, D_MODEL, 2 * D_RNN), D_MODEL),
        'conv_rg_w': nrm(ks[10], (N_A, RG_CONV, D_RNN), RG_CONV),
        'conv_rg_b': 0.02 * jax.random.normal(ks[11], (N_A, D_RNN), f32),
        'w_rg_a': nrm(ks[12], (N_A, N_RNN_HEADS, RNN_BLOCK, RNN_BLOCK), RNN_BLOCK),
        'b_rg_a': 0.02 * jax.random.normal(ks[13], (N_A, D_RNN), f32),
        'w_rg_i': nrm(ks[14], (N_A, N_RNN_HEADS, RNN_BLOCK, RNN_BLOCK), RNN_BLOCK),
        'b_rg_i': 0.02 * jax.random.normal(ks[15], (N_A, D_RNN), f32),
        'rg_lambda': rg_lambda,
        'w_rg_o': nrm(ks[16], (N_A, D_RNN, D_MODEL), D_RNN),
        'kv_in_g': 1.0 + 0.05 * jax.random.normal(ks[17], (D_MODEL,), f32),
        'w_dkv': nrm(ks[18], (D_MODEL, KV_LORA + ROPE_DIM), D_MODEL),
        'kv_norm_g': 1.0 + 0.05 * jax.random.normal(ks[19], (KV_LORA,), f32),
        'w_uk': nrm(ks[21], (KV_LORA, N_HEADS, NOPE_DIM), KV_LORA),
        'w_uv': nrm(ks[22], (KV_LORA, N_HEADS, V_DIM), KV_LORA),
        'w_dq': nrm(ks[23], (N_B, D_MODEL, Q_LORA), D_MODEL),
        'q_norm_g': 1.0 + 0.05 * jax.random.normal(ks[24], (N_B, Q_LORA), f32),
        'w_uq': nrm(ks[25], (N_B, Q_LORA, N_HEADS, NOPE_DIM + ROPE_DIM), Q_LORA),
        'w_mla_o': nrm(ks[26], (N_B, N_HEADS * V_DIM, D_MODEL), N_HEADS * V_DIM),
        'w_ffn_g': nrm(ks[27], (DEPTH, D_MODEL, D_FF), D_MODEL),
        'w_ffn_u': nrm(ks[28], (DEPTH, D_MODEL, D_FF), D_MODEL),
        'ffn_conv_w': nrm(ks[29], (DEPTH, FFN_CONV, D_FF), FFN_CONV),
        'ffn_conv_b': 0.02 * jax.random.normal(ks[30], (DEPTH, D_FF), f32),
        'w_ffn_d': nrm(ks[31], (DEPTH, D_FF, D_MODEL), D_FF),
    }


def reference(x_prompt, x_sample, state_rglru_h, state_rglru_conv, state_ffn_conv,
              cache_kv_latent, cache_k_rope, page_table, norm_g, w_rg_in, conv_rg_w,
              conv_rg_b, w_rg_a, b_rg_a, w_rg_i, b_rg_i, rg_lambda, w_rg_o, kv_in_g, w_dkv,
              kv_norm_g, w_uk, w_uv, w_dq, q_norm_g, w_uq, w_mla_o, w_ffn_g, w_ffn_u,
              ffn_conv_w, ffn_conv_b, w_ffn_d):
    P = {'norm_g': norm_g, 'w_rg_in': w_rg_in, 'conv_rg_w': conv_rg_w, 'conv_rg_b': conv_rg_b,
         'w_rg_a': w_rg_a, 'b_rg_a': b_rg_a, 'w_rg_i': w_rg_i, 'b_rg_i': b_rg_i,
         'rg_lambda': rg_lambda, 'w_rg_o': w_rg_o, 'kv_in_g': kv_in_g, 'w_dkv': w_dkv,
         'kv_norm_g': kv_norm_g, 'w_uk': w_uk, 'w_uv': w_uv, 'w_dq': w_dq,
         'q_norm_g': q_norm_g, 'w_uq': w_uq, 'w_mla_o': w_mla_o, 'w_ffn_g': w_ffn_g,
         'w_ffn_u': w_ffn_u, 'ffn_conv_w': ffn_conv_w, 'ffn_conv_b': ffn_conv_b,
         'w_ffn_d': w_ffn_d}
    bsz, s = x_prompt.shape[0], x_prompt.shape[1]
    dbsz, t = x_sample.shape[0], x_sample.shape[1]

    pos_p = jnp.arange(s, dtype=jnp.float32)
    h0_p = jnp.zeros((N_A, bsz, D_RNN), jnp.float32)
    rgb0_p = jnp.zeros((N_A, bsz, RG_CONV - 1, D_RNN), x_prompt.dtype)
    ffb0_p = jnp.zeros((DEPTH, bsz, FFN_CONV - 1, D_FF), x_prompt.dtype)
    y_prompt, p_rglru_h, p_rglru_conv, p_ffn_conv, p_kv_latent, p_k_rope = trunk(
        x_prompt, pos_p, h0_p, rgb0_p, ffb0_p, prompt_attention, P)

    c_past = cache_kv_latent[page_table].reshape(dbsz, -1, KV_LORA)
    kr_past = cache_k_rope[page_table].reshape(dbsz, -1, ROPE_DIM)
    past_len = c_past.shape[1]
    pos_s = past_len + jnp.arange(t, dtype=jnp.float32)

    def attend_sample(q_lat, q_rope, c_new, kr_new):
        return sample_attention(q_lat, q_rope, c_new, kr_new, c_past, kr_past)

    y_sample, s_rglru_h, s_rglru_conv, s_ffn_conv, s_kv_latent, s_k_rope = trunk(
        x_sample, pos_s, state_rglru_h, state_rglru_conv, state_ffn_conv, attend_sample, P)

    return (y_prompt, y_sample, p_rglru_h, s_rglru_h, p_rglru_conv, s_rglru_conv,
            p_ffn_conv, s_ffn_conv, p_kv_latent, s_kv_latent, p_k_rope, s_k_rope)
```

```python
import functools
import math

import jax
import jax.numpy as jnp
from jax import lax
from jax.experimental import pallas as pl
from jax.experimental.pallas import tpu as pltpu

F32 = jnp.float32
BF16 = jnp.bfloat16

EPS = 1e-6
RG_C = 8.0
N_RNN_HEADS = 8
RG_CONV = 4
FFN_CONV = 3
N_HEADS = 16
KV_LORA = 512
NOPE_DIM = 128
ROPE_DIM = 64
V_DIM = 128
ROPE_THETA = 10000.0
ATTN_SCALE = 1.0 / math.sqrt(NOPE_DIM + ROPE_DIM)

V7X_LANES = 128
V7X_SUBLANES = 8
V7X_VMEM_BYTES = 64 * 1024 * 1024
VMEM_LIMIT = V7X_VMEM_BYTES - 8 * 1024 * 1024

ROW_TILE = 512
FF_TILE = 512
K_TILE = 1024
SCAN_TILE = 256
Q_TILE = 512
PAGES_PER_STEP = 16


def _params(*sem):
    return pltpu.CompilerParams(dimension_semantics=sem, vmem_limit_bytes=VMEM_LIMIT)


def _dot(a, b):
    return jnp.dot(a, b, preferred_element_type=F32)


def _dot_nt(a, b):
    return lax.dot_general(a, b, (((1,), (1,)), ((), ())), preferred_element_type=F32)


def _rms(v, g):
    return v * lax.rsqrt(jnp.mean(v * v, axis=-1, keepdims=True) + EPS) * g


def _gelu(v):
    return jax.nn.gelu(v, approximate=True)


def _rope_pairs(t):
    return t + pltpu.roll(t, ROPE_DIM, axis=1)


def _norm_kernel(x_ref, g_ref, o_ref):
    o_ref[...] = _rms(x_ref[...], g_ref[...]).astype(o_ref.dtype)


def _norm(x, g, tm):
    m, d = x.shape
    return pl.pallas_call(
        _norm_kernel,
        grid=(m // tm,),
        in_specs=[pl.BlockSpec((tm, d), lambda i: (i, 0)),
                  pl.BlockSpec((1, d), lambda i: (0, 0))],
        out_specs=pl.BlockSpec((tm, d), lambda i: (i, 0)),
        out_shape=jax.ShapeDtypeStruct((m, d), BF16),
        compiler_params=_params("parallel"),
        name="rmsnorm",
    )(x, g.reshape(1, d))


def _mm_kernel(a_ref, w_ref, o_ref):
    o_ref[...] = _dot(a_ref[...], w_ref[...]).astype(o_ref.dtype)


def _mm(a, w, tm, tn, out_dtype):
    m, k = a.shape
    n = w.shape[1]
    return pl.pallas_call(
        _mm_kernel,
        grid=(n // tn, m // tm),
        in_specs=[pl.BlockSpec((tm, k), lambda j, i: (i, 0)),
                  pl.BlockSpec((k, tn), lambda j, i: (0, j))],
        out_specs=pl.BlockSpec((tm, tn), lambda j, i: (i, j)),
        out_shape=jax.ShapeDtypeStruct((m, n), out_dtype),
        compiler_params=_params("parallel", "parallel"),
        name="matmul",
    )(a, w)


def _mm_resnorm_kernel(*refs, emit_next):
    if emit_next:
        a_ref, w_ref, x_ref, gp_ref, gn_ref, xo_ref, xn_ref, acc_ref = refs
    else:
        a_ref, w_ref, x_ref, gp_ref, xo_ref, acc_ref = refs
    k = pl.program_id(1)

    @pl.when(k == 0)
    def _():
        acc_ref[...] = jnp.zeros_like(acc_ref)

    acc_ref[...] += _dot(a_ref[...], w_ref[...])

    @pl.when(k == pl.num_programs(1) - 1)
    def _():
        xo = x_ref[...] + _rms(acc_ref[...], gp_ref[...])
        xo_ref[...] = xo
        if emit_next:
            xn_ref[...] = _rms(xo, gn_ref[...]).astype(xn_ref.dtype)


def _mm_resnorm(a, w, x, g_post, g_next, tm):
    m, kdim = a.shape
    d = w.shape[1]
    tk = min(K_TILE, kdim)
    emit_next = g_next is not None
    row = lambda i, k: (i, 0)
    vec = pl.BlockSpec((1, d), lambda i, k: (0, 0))
    in_specs = [pl.BlockSpec((tm, tk), lambda i, k: (i, k)),
                pl.BlockSpec((tk, d), lambda i, k: (k, 0)),
                pl.BlockSpec((tm, d), row), vec]
    args = [a, w, x, g_post.reshape(1, d)]
    out_specs = [pl.BlockSpec((tm, d), row)]
    out_shape = [jax.ShapeDtypeStruct((m, d), F32)]
    if emit_next:
        in_specs.append(vec)
        args.append(g_next.reshape(1, d))
        out_specs.append(pl.BlockSpec((tm, d), row))
        out_shape.append(jax.ShapeDtypeStruct((m, d), BF16))
    outs = pl.pallas_call(
        functools.partial(_mm_resnorm_kernel, emit_next=emit_next),
        grid=(m // tm, kdim // tk),
        in_specs=in_specs,
        out_specs=out_specs,
        out_shape=out_shape,
        scratch_shapes=[pltpu.VMEM((tm, d), F32)],
        compiler_params=_params("parallel", "arbitrary"),
        name="matmul_resnorm",
    )(*args)
    return (outs[0], outs[1]) if emit_next else (outs[0], None)


def _mm_norm_kernel(a_ref, w_ref, g_ref, o_ref):
    o_ref[...] = _rms(_dot(a_ref[...], w_ref[...]), g_ref[...]).astype(o_ref.dtype)


def _mm_norm(a, w, g, tm):
    m, k = a.shape
    n = w.shape[1]
    return pl.pallas_call(
        _mm_norm_kernel,
        grid=(m // tm,),
        in_specs=[pl.BlockSpec((tm, k), lambda i: (i, 0)),
                  pl.BlockSpec((k, n), lambda i: (0, 0)),
                  pl.BlockSpec((1, n), lambda i: (0, 0))],
        out_specs=pl.BlockSpec((tm, n), lambda i: (i, 0)),
        out_shape=jax.ShapeDtypeStruct((m, n), BF16),
        compiler_params=_params("parallel"),
        name="matmul_norm",
    )(a, w, g.reshape(1, n))


def _ffn_gate(g, p1, p2, u, cw_ref, cb_ref):
    cw = cw_ref[...]
    gc = cb_ref[...] + p2 * cw[0:1] + p1 * cw[1:2] + g * cw[2:3]
    return _gelu(gc) * u


def _ffn_gu_seq_kernel(xn_ref, wg_ref, wu_ref, cw_ref, cb_ref, h_ref, buf_ref, s_ref,
                       *, tm, tiles_per_seq):
    t = pl.program_id(1) % tiles_per_seq
    hist = V7X_SUBLANES

    @pl.when(t == 0)
    def _():
        s_ref[0:hist, :] = jnp.zeros((hist, s_ref.shape[1]), F32)

    xn = xn_ref[...]
    g = _dot(xn, wg_ref[...])
    u = _dot(xn, wu_ref[...])
    s_ref[hist:hist + tm, :] = g
    p1 = s_ref[hist - 1:hist - 1 + tm, :]
    p2 = s_ref[hist - 2:hist - 2 + tm, :]
    h_ref[...] = _ffn_gate(g, p1, p2, u, cw_ref, cb_ref).astype(h_ref.dtype)
    s_ref[0:hist, :] = s_ref[tm:tm + hist, :]

    @pl.when(t == tiles_per_seq - 1)
    def _():
        buf_ref[0] = s_ref[hist - (FFN_CONV - 1):hist, :]


def _ffn_gu_seq(xn, wg, wu, cw, cb, n_seq):
    m, d = xn.shape
    dff = wg.shape[1]
    tm, tn = ROW_TILE, FF_TILE
    tps = m // n_seq // tm
    return pl.pallas_call(
        functools.partial(_ffn_gu_seq_kernel, tm=tm, tiles_per_seq=tps),
        grid=(dff // tn, m // tm),
        in_specs=[pl.BlockSpec((tm, d), lambda j, i: (i, 0)),
                  pl.BlockSpec((d, tn), lambda j, i: (0, j)),
                  pl.BlockSpec((d, tn), lambda j, i: (0, j)),
                  pl.BlockSpec((FFN_CONV, tn), lambda j, i: (0, j)),
                  pl.BlockSpec((1, tn), lambda j, i: (0, j))],
        out_specs=[pl.BlockSpec((tm, tn), lambda j, i: (i, j)),
                   pl.BlockSpec((1, FFN_CONV - 1, tn), lambda j, i: (i // tps, 0, j))],
        out_shape=[jax.ShapeDtypeStruct((m, dff), BF16),
                   jax.ShapeDtypeStruct((n_seq, FFN_CONV - 1, dff), F32)],
        scratch_shapes=[pltpu.VMEM((tm + V7X_SUBLANES, tn), F32)],
        compiler_params=_params("parallel", "arbitrary"),
        name="ffn_gate_up_seq",
    )(xn, wg, wu, cw, cb.reshape(1, dff))


def _ffn_gu_step_kernel(xn_ref, wg_ref, wu_ref, cw_ref, cb_ref, p2_ref, p1_ref, h_ref, g_ref):
    xn = xn_ref[...]
    g = _dot(xn, wg_ref[...])
    u = _dot(xn, wu_ref[...])
    g_ref[...] = g
    h_ref[...] = _ffn_gate(g, p1_ref[...], p2_ref[...], u, cw_ref, cb_ref).astype(h_ref.dtype)


def _ffn_gu_step(xn, wg, wu, cw, cb, p2, p1):
    m, d = xn.shape
    dff = wg.shape[1]
    tn = FF_TILE
    col = lambda j: (0, j)
    return pl.pallas_call(
        _ffn_gu_step_kernel,
        grid=(dff // tn,),
        in_specs=[pl.BlockSpec((m, d), lambda j: (0, 0)),
                  pl.BlockSpec((d, tn), col), pl.BlockSpec((d, tn), col),
                  pl.BlockSpec((FFN_CONV, tn), col), pl.BlockSpec((1, tn), col),
                  pl.BlockSpec((m, tn), col), pl.BlockSpec((m, tn), col)],
        out_specs=[pl.BlockSpec((m, tn), col), pl.BlockSpec((m, tn), col)],
        out_shape=[jax.ShapeDtypeStruct((m, dff), BF16), jax.ShapeDtypeStruct((m, dff), F32)],
        compiler_params=_params("parallel"),
        name="ffn_gate_up_step",
    )(xn, wg, wu, cw, cb.reshape(1, dff), p2, p1)


def _block_diag(xb16, w_ref):
    bs = xb16.shape[1] // N_RNN_HEADS
    return jnp.concatenate(
        [_dot(xb16[:, h * bs:(h + 1) * bs], w_ref[h]) for h in range(N_RNN_HEADS)], axis=1)


def _rg_gates(xc, wa_ref, ba_ref, wi_ref, bi_ref, lam_ref):
    xb16 = xc.astype(BF16)
    r = jax.nn.sigmoid(_block_diag(xb16, wa_ref) + ba_ref[...])
    i = jax.nn.sigmoid(_block_diag(xb16, wi_ref) + bi_ref[...])
    log_a = -RG_C * r * jax.nn.softplus(-lam_ref[...])
    a = jnp.exp(log_a)
    u = jnp.sqrt(-jnp.tanh(log_a) * (a * a + 1.0)) * (i * xc)
    return a, u


def _rglru_seq_kernel(xb_ref, yb_ref, cw_ref, cb_ref, wa_ref, ba_ref, wi_ref, bi_ref, lam_ref,
                      z_ref, hl_ref, nb_ref, s_ref, a_ref, u_ref, h_ref, *, tt):
    t = pl.program_id(1)
    hist = V7X_SUBLANES

    @pl.when(t == 0)
    def _():
        s_ref[0:hist, :] = jnp.zeros((hist, s_ref.shape[1]), F32)
        h_ref[...] = jnp.zeros_like(h_ref)

    xb = xb_ref[...]
    s_ref[hist:hist + tt, :] = xb
    cw = cw_ref[...]
    xc = cb_ref[...] + s_ref[hist - 3:hist - 3 + tt, :] * cw[0:1]
    xc = xc + s_ref[hist - 2:hist - 2 + tt, :] * cw[1:2]
    xc = xc + s_ref[hist - 1:hist - 1 + tt, :] * cw[2:3]
    xc = xc + xb * cw[3:4]
    a, u = _rg_gates(xc, wa_ref, ba_ref, wi_ref, bi_ref, lam_ref)
    a_ref[...] = a
    u_ref[...] = u

    def step(k, h):
        h = a_ref[pl.ds(k, 1), :] * h + u_ref[pl.ds(k, 1), :]
        u_ref[pl.ds(k, 1), :] = h
        return h

    h_ref[...] = lax.fori_loop(0, tt, step, h_ref[...], unroll=8)
    z_ref[...] = (_gelu(yb_ref[...]) * u_ref[...]).astype(z_ref.dtype)
    s_ref[0:hist, :] = s_ref[tt:tt + hist, :]

    @pl.when(t == pl.num_programs(1) - 1)
    def _():
        hl_ref[0] = h_ref[...]
        nb_ref[0] = s_ref[hist - (RG_CONV - 1):hist, :]


def _rglru_seq(proj, cw, cb, wa, ba, wi, bi, lam, n_seq):
    m = proj.shape[0]
    c = proj.shape[1] // 2
    tt = SCAN_TILE
    nt = m // n_seq // tt
    bs = c // N_RNN_HEADS
    vec = pl.BlockSpec((1, c), lambda b, t: (0, 0))
    blk = pl.BlockSpec((N_RNN_HEADS, bs, bs), lambda b, t: (0, 0, 0))
    return pl.pallas_call(
        functools.partial(_rglru_seq_kernel, tt=tt),
        grid=(n_seq, nt),
        in_specs=[pl.BlockSpec((tt, c), lambda b, t: (b * nt + t, 0)),
                  pl.BlockSpec((tt, c), lambda b, t: (b * nt + t, 1)),
                  pl.BlockSpec((RG_CONV, c), lambda b, t: (0, 0)), vec,
                  blk, vec, blk, vec, vec],
        out_specs=[pl.BlockSpec((tt, c), lambda b, t: (b * nt + t, 0)),
                   pl.BlockSpec((1, 1, c), lambda b, t: (b, 0, 0)),
                   pl.BlockSpec((1, RG_CONV - 1, c), lambda b, t: (b, 0, 0))],
        out_shape=[jax.ShapeDtypeStruct((m, c), BF16),
                   jax.ShapeDtypeStruct((n_seq, 1, c), F32),
                   jax.ShapeDtypeStruct((n_seq, RG_CONV - 1, c), F32)],
        scratch_shapes=[pltpu.VMEM((tt + V7X_SUBLANES, c), F32),
                        pltpu.VMEM((tt, c), F32), pltpu.VMEM((tt, c), F32),
                        pltpu.VMEM((1, c), F32)],
        compiler_params=_params("parallel", "arbitrary"),
        name="rglru_seq",
    )(proj, proj, cw, cb.reshape(1, c), wa, ba.reshape(1, c), wi, bi.reshape(1, c),
      lam.reshape(1, c))


def _rglru_step_kernel(xb_ref, yb_ref, b0_ref, b1_ref, b2_ref, h0_ref, cw_ref, cb_ref,
                       wa_ref, ba_ref, wi_ref, bi_ref, lam_ref, z_ref, h_ref):
    cw = cw_ref[...]
    xb = xb_ref[...]
    xc = cb_ref[...] + b0_ref[...] * cw[0:1]
    xc = xc + b1_ref[...] * cw[1:2]
    xc = xc + b2_ref[...] * cw[2:3]
    xc = xc + xb * cw[3:4]
    a, u = _rg_gates(xc, wa_ref, ba_ref, wi_ref, bi_ref, lam_ref)
    h = a * h0_ref[...] + u
    h_ref[...] = h
    z_ref[...] = (_gelu(yb_ref[...]) * h).astype(z_ref.dtype)


def _rglru_step(proj, b0, b1, b2, h0, cw, cb, wa, ba, wi, bi, lam):
    m = proj.shape[0]
    c = proj.shape[1] // 2
    bs = c // N_RNN_HEADS
    full = pl.BlockSpec((m, c), lambda i: (0, 0))
    vec = pl.BlockSpec((1, c), lambda i: (0, 0))
    blk = pl.BlockSpec((N_RNN_HEADS, bs, bs), lambda i: (0, 0, 0))
    return pl.pallas_call(
        _rglru_step_kernel,
        grid=(1,),
        in_specs=[full, pl.BlockSpec((m, c), lambda i: (0, 1)), full, full, full, full,
                  pl.BlockSpec((RG_CONV, c), lambda i: (0, 0)), vec, blk, vec, blk, vec, vec],
        out_specs=[full, full],
        out_shape=[jax.ShapeDtypeStruct((m, c), BF16), jax.ShapeDtypeStruct((m, c), F32)],
        compiler_params=_params("arbitrary"),
        name="rglru_step",
    )(proj, proj, b0, b1, b2, h0, cw, cb.reshape(1, c), wa, ba.reshape(1, c), wi,
      bi.reshape(1, c), lam.reshape(1, c))


def _kv_kernel(x_ref, gin_ref, w_ref, gkv_ref, cs_ref, c_ref, kr_ref, cb_ref, krb_ref):
    xn = _rms(x_ref[...], gin_ref[...]).astype(BF16)
    kv = _dot(xn, w_ref[...])
    c = _rms(kv[:, :KV_LORA], gkv_ref[...])
    r = _rope_pairs(kv[:, KV_LORA:] * cs_ref[...])
    c_ref[...] = c
    kr_ref[...] = r[:, :ROPE_DIM]
    cb_ref[...] = c.astype(BF16)
    lane = lax.broadcasted_iota(jnp.int32, r.shape, 1)
    krb_ref[...] = jnp.where(lane < ROPE_DIM, r, 0.0).astype(BF16)


def _kv(x, g_in, w_ext, g_kv, cs, tm):
    m, d = x.shape
    n = w_ext.shape[1]
    cs_tiles = cs.shape[0] // tm
    row = lambda i: (i, 0)
    return pl.pallas_call(
        _kv_kernel,
        grid=(m // tm,),
        in_specs=[pl.BlockSpec((tm, d), row),
                  pl.BlockSpec((1, d), lambda i: (0, 0)),
                  pl.BlockSpec((d, n), lambda i: (0, 0)),
                  pl.BlockSpec((1, KV_LORA), lambda i: (0, 0)),
                  pl.BlockSpec((tm, V7X_LANES), lambda i: (i % cs_tiles, 0))],
        out_specs=[pl.BlockSpec((tm, KV_LORA), row), pl.BlockSpec((tm, ROPE_DIM), row),
                   pl.BlockSpec((tm, KV_LORA), row), pl.BlockSpec((tm, V7X_LANES), row)],
        out_shape=[jax.ShapeDtypeStruct((m, KV_LORA), F32),
                   jax.ShapeDtypeStruct((m, ROPE_DIM), F32),
                   jax.ShapeDtypeStruct((m, KV_LORA), BF16),
                   jax.ShapeDtypeStruct((m, V7X_LANES), BF16)],
        compiler_params=_params("parallel"),
        name="shared_kv",
    )(x, g_in.reshape(1, d), w_ext, g_kv.reshape(1, KV_LORA), cs)


def _q_kernel(cq_ref, wq_ref, wk_ref, cs_ref, ql_ref, qr_ref):
    q = _dot(cq_ref[...], wq_ref[0])
    ql_ref[...] = _dot(q[:, :NOPE_DIM].astype(BF16), wk_ref[0]).astype(ql_ref.dtype)
    qr_ref[...] = _rope_pairs(q[:, NOPE_DIM:] * cs_ref[...]).astype(qr_ref.dtype)


def _q_proj(cq, wq_ext, wk_t, cs, tm):
    m, ql = cq.shape
    cs_tiles = cs.shape[0] // tm
    return pl.pallas_call(
        _q_kernel,
        grid=(m // tm, N_HEADS),
        in_specs=[pl.BlockSpec((tm, ql), lambda i, h: (i, 0)),
                  pl.BlockSpec((1, ql, 2 * V7X_LANES), lambda i, h: (h, 0, 0)),
                  pl.BlockSpec((1, NOPE_DIM, KV_LORA), lambda i, h: (h, 0, 0)),
                  pl.BlockSpec((tm, V7X_LANES), lambda i, h: (i % cs_tiles, 0))],
        out_specs=[pl.BlockSpec((tm, KV_LORA), lambda i, h: (i, h)),
                   pl.BlockSpec((tm, V7X_LANES), lambda i, h: (i, h))],
        out_shape=[jax.ShapeDtypeStruct((m, N_HEADS * KV_LORA), BF16),
                   jax.ShapeDtypeStruct((m, N_HEADS * V7X_LANES), BF16)],
        compiler_params=_params("parallel", "parallel"),
        name="mla_query",
    )(cq, wq_ext, wk_t, cs)


def _softmax_init(m_ref, l_ref, acc_ref):
    m_ref[...] = jnp.full(m_ref.shape, -jnp.inf, F32)
    l_ref[...] = jnp.zeros_like(l_ref)
    acc_ref[...] = jnp.zeros_like(acc_ref)


def _softmax_update(s, v, m_ref, l_ref, acc_ref):
    m_old = m_ref[...]
    m_new = jnp.maximum(m_old, jnp.max(s, axis=-1, keepdims=True))
    alpha = jnp.exp(m_old - m_new)
    p = jnp.exp(s - m_new)
    l_ref[...] = alpha * l_ref[...] + jnp.sum(p, axis=-1, keepdims=True)
    acc_ref[...] = alpha * acc_ref[...] + _dot(p.astype(BF16), v)
    m_ref[...] = m_new


def _attn_prompt_kernel(ql_ref, qr_ref, c_ref, kr_ref, wuv_ref, o_ref, m_ref, l_ref, acc_ref,
                        *, tq):
    qi = pl.program_id(2)
    ql = ql_ref[...]
    qr = qr_ref[...]
    _softmax_init(m_ref, l_ref, acc_ref)

    def chunk(j, masked):
        off = pl.multiple_of(j * tq, tq)
        c = c_ref[pl.ds(off, tq), :]
        kr = kr_ref[pl.ds(off, tq), :]
        s = (_dot_nt(ql, c) + _dot_nt(qr, kr)) * ATTN_SCALE
        if masked:
            row = lax.broadcasted_iota(jnp.int32, s.shape, 0)
            col = lax.broadcasted_iota(jnp.int32, s.shape, 1)
            s = jnp.where(col <= row, s, -jnp.inf)
        _softmax_update(s, c, m_ref, l_ref, acc_ref)

    def body(j, carry):
        chunk(j, False)
        return carry

    lax.fori_loop(0, qi, body, 0)
    chunk(qi, True)
    o = (acc_ref[...] / l_ref[...]).astype(BF16)
    o_ref[...] = _dot(o, wuv_ref[0]).astype(o_ref.dtype)


def _attn_prompt(ql, qr, cb, krb, wuv, n_seq):
    m = ql.shape[0]
    s = m // n_seq
    tq = Q_TILE
    nq = s // tq
    return pl.pallas_call(
        functools.partial(_attn_prompt_kernel, tq=tq),
        grid=(n_seq, N_HEADS, nq),
        in_specs=[pl.BlockSpec((tq, KV_LORA), lambda b, h, q: (b * nq + q, h)),
                  pl.BlockSpec((tq, V7X_LANES), lambda b, h, q: (b * nq + q, h)),
                  pl.BlockSpec((s, KV_LORA), lambda b, h, q: (b, 0)),
                  pl.BlockSpec((s, V7X_LANES), lambda b, h, q: (b, 0)),
                  pl.BlockSpec((1, KV_LORA, V_DIM), lambda b, h, q: (h, 0, 0))],
        out_specs=pl.BlockSpec((tq, V_DIM), lambda b, h, q: (b * nq + q, h)),
        out_shape=jax.ShapeDtypeStruct((m, N_HEADS * V_DIM), BF16),
        scratch_shapes=[pltpu.VMEM((tq, 1), F32), pltpu.VMEM((tq, 1), F32),
                        pltpu.VMEM((tq, KV_LORA), F32)],
        compiler_params=_params("parallel", "parallel", "arbitrary"),
        name="attn_prompt",
    )(ql, qr, cb, krb, wuv)


def _attn_sample_kernel(pt_ref, ql_ref, qr_ref, cn_ref, krn_ref, *rest, pages, page_size):
    lat_refs = rest[:pages]
    rope_refs = rest[pages:2 * pages]
    o_ref, cbf_ref, krbf_ref, m_ref, l_ref, acc_ref = rest[2 * pages:]
    j = pl.program_id(1)

    @pl.when(j == 0)
    def _():
        _softmax_init(m_ref, l_ref, acc_ref)

    for k in range(pages):
        rows = slice(k * page_size, (k + 1) * page_size)
        cbf_ref[rows, :] = lat_refs[k][0].astype(BF16)
        krbf_ref[rows, :] = rope_refs[k][0].astype(BF16)

    ql = ql_ref[0]
    qr = qr_ref[0][:, :ROPE_DIM]
    c = cbf_ref[...]
    s = (_dot_nt(ql, c) + _dot_nt(qr, krbf_ref[...])) * ATTN_SCALE
    _softmax_update(s, c, m_ref, l_ref, acc_ref)

    @pl.when(j == pl.num_programs(1) - 1)
    def _():
        cn = cn_ref[0].astype(F32)
        krn = krn_ref[0][:, :ROPE_DIM].astype(F32)
        s_new = (jnp.sum(ql.astype(F32) * cn, axis=-1, keepdims=True)
                 + jnp.sum(qr.astype(F32) * krn, axis=-1, keepdims=True)) * ATTN_SCALE
        m_old = m_ref[...]
        m_new = jnp.maximum(m_old, s_new)
        alpha = jnp.exp(m_old - m_new)
        p_new = jnp.exp(s_new - m_new)
        l = alpha * l_ref[...] + p_new
        acc = alpha * acc_ref[...] + p_new.astype(BF16).astype(F32) * cn
        o_ref[0] = (acc / l).astype(o_ref.dtype)


def _attn_sample(page_table, ql, qr, cb_new, krb_new, cache_lat, cache_rope):
    nb, n_pages = page_table.shape
    page_size = cache_lat.shape[1]
    pages = PAGES_PER_STEP
    nch = n_pages // pages
    pt = page_table.reshape(nb * n_pages)

    def page_map(k):
        return lambda b, j, pt_ref: (pt_ref[b * n_pages + j * pages + k], 0, 0)

    per_seq = lambda b, j, pt_ref: (b, 0, 0)
    in_specs = [pl.BlockSpec((1, N_HEADS, KV_LORA), per_seq),
                pl.BlockSpec((1, N_HEADS, V7X_LANES), per_seq),
                pl.BlockSpec((1, 1, KV_LORA), per_seq),
                pl.BlockSpec((1, 1, V7X_LANES), per_seq)]
    in_specs += [pl.BlockSpec((1, page_size, KV_LORA), page_map(k)) for k in range(pages)]
    in_specs += [pl.BlockSpec((1, page_size, ROPE_DIM), page_map(k)) for k in range(pages)]
    chunk = pages * page_size
    return pl.pallas_call(
        functools.partial(_attn_sample_kernel, pages=pages, page_size=page_size),
        grid_spec=pltpu.PrefetchScalarGridSpec(
            num_scalar_prefetch=1,
            grid=(nb, nch),
            in_specs=in_specs,
            out_specs=pl.BlockSpec((1, N_HEADS, KV_LORA), per_seq),
            scratch_shapes=[pltpu.VMEM((chunk, KV_LORA), BF16),
                            pltpu.VMEM((chunk, ROPE_DIM), BF16),
                            pltpu.VMEM((N_HEADS, 1), F32), pltpu.VMEM((N_HEADS, 1), F32),
                            pltpu.VMEM((N_HEADS, KV_LORA), F32)]),
        out_shape=jax.ShapeDtypeStruct((nb, N_HEADS, KV_LORA), BF16),
        compiler_params=_params("parallel", "arbitrary"),
        name="attn_sample",
    )(pt, ql.reshape(nb, N_HEADS, KV_LORA), qr.reshape(nb, N_HEADS, V7X_LANES),
      cb_new.reshape(nb, 1, KV_LORA), krb_new.reshape(nb, 1, V7X_LANES),
      *([cache_lat] * pages), *([cache_rope] * pages))


def _uv_kernel(o_ref, w_ref, out_ref):
    out_ref[...] = _dot(o_ref[...], w_ref[0]).astype(out_ref.dtype)


def _uv(o_lat, wuv):
    m = o_lat.shape[0]
    return pl.pallas_call(
        _uv_kernel,
        grid=(N_HEADS,),
        in_specs=[pl.BlockSpec((m, KV_LORA), lambda h: (0, h)),
                  pl.BlockSpec((1, KV_LORA, V_DIM), lambda h: (h, 0, 0))],
        out_specs=pl.BlockSpec((m, V_DIM), lambda h: (0, h)),
        out_shape=jax.ShapeDtypeStruct((m, N_HEADS * V_DIM), BF16),
        compiler_params=_params("parallel"),
        name="value_up",
    )(o_lat.reshape(m, N_HEADS * KV_LORA), wuv)


def _rope_table(pos):
    half = ROPE_DIM // 2
    inv = ROPE_THETA ** (-jnp.arange(half, dtype=F32) * 2.0 / ROPE_DIM)
    ang = pos[:, None] * inv[None, :]
    cos, sin = jnp.cos(ang), jnp.sin(ang)
    return jnp.concatenate([cos, cos, sin, sin], axis=-1)


def _with_rotated(w_rope):
    half = ROPE_DIM // 2
    w1, w2 = w_rope[..., :half], w_rope[..., half:]
    return jnp.concatenate([w1, w2, -w2, w1], axis=-1)


def _prep_weights(p):
    w = dict(p)
    for name in ("w_rg_in", "w_rg_a", "w_rg_i", "w_rg_o", "w_dq", "w_mla_o",
                 "w_ffn_g", "w_ffn_u", "w_ffn_d"):
        w[name] = p[name].astype(BF16)
    w["w_dkv_ext"] = jnp.concatenate(
        [p["w_dkv"][:, :KV_LORA], _with_rotated(p["w_dkv"][:, KV_LORA:])], axis=-1).astype(BF16)
    uq = jnp.swapaxes(p["w_uq"], 1, 2)
    w["w_uq_ext"] = jnp.concatenate(
        [uq[..., :NOPE_DIM], _with_rotated(uq[..., NOPE_DIM:])], axis=-1).astype(BF16)
    w["w_uk_t"] = jnp.transpose(p["w_uk"], (1, 2, 0)).astype(BF16)
    w["w_uv_h"] = jnp.transpose(p["w_uv"], (1, 0, 2)).astype(BF16)
    return w


def _trunk_prompt(x3, w):
    n_seq, s, d = x3.shape
    depth = w["norm_g"].shape[0]
    n_a = w["w_rg_in"].shape[0]
    tm = ROW_TILE
    cs = _rope_table(jnp.arange(s, dtype=F32))
    x = x3.reshape(n_seq * s, d)
    xn = _norm(x, w["norm_g"][0, 0], tm)
    hs, rg_bufs, ffn_bufs = [], [], []
    for l in range(depth):
        g = w["norm_g"][l]
        if l < n_a:
            proj = _mm(xn, w["w_rg_in"][l], tm, 1024, F32)
            z, hl, nb = _rglru_seq(proj, w["conv_rg_w"][l], w["conv_rg_b"][l], w["w_rg_a"][l],
                                   w["b_rg_a"][l], w["w_rg_i"][l], w["b_rg_i"][l],
                                   w["rg_lambda"][l], n_seq)
            hs.append(hl[:, 0])
            rg_bufs.append(nb)
            x, xn = _mm_resnorm(z, w["w_rg_o"][l], x, g[1], g[2], tm)
        else:
            j = l - n_a
            cq = _mm_norm(xn, w["w_dq"][j], w["q_norm_g"][j], tm)
            ql, qr = _q_proj(cq, w["w_uq_ext"][j], w["w_uk_t"], cs, tm)
            o = _attn_prompt(ql, qr, cb, krb, w["w_uv_h"], n_seq)
            x, xn = _mm_resnorm(o, w["w_mla_o"][j], x, g[1], g[2], tm)
        h, fb = _ffn_gu_seq(xn, w["w_ffn_g"][l], w["w_ffn_u"][l], w["ffn_conv_w"][l],
                            w["ffn_conv_b"][l], n_seq)
        ffn_bufs.append(fb)
        g_next = w["norm_g"][l + 1, 0] if l + 1 < depth else None
        x, xn = _mm_resnorm(h, w["w_ffn_d"][l], x, g[3], g_next, tm)
        if l == n_a - 1:
            c, kr, cb, krb = _kv(x, w["kv_in_g"], w["w_dkv_ext"], w["kv_norm_g"], cs, tm)
    return (x.reshape(n_seq, s, d), jnp.stack(hs), jnp.stack(rg_bufs), jnp.stack(ffn_bufs),
            c.reshape(n_seq, s, KV_LORA), kr.reshape(n_seq, s, ROPE_DIM))


def _trunk_sample(x3, rg_h0, rg_buf0, ffn_buf0, cache_lat, cache_rope, page_table, w):
    nb, t, d = x3.shape
    assert t == 1, "the sample group advances one token per sequence"
    depth = w["norm_g"].shape[0]
    n_a = w["w_rg_in"].shape[0]
    past_len = page_table.shape[1] * cache_lat.shape[1]
    cs = _rope_table(jnp.full((nb,), past_len, dtype=F32))
    x = x3.reshape(nb, d)
    xn = _norm(x, w["norm_g"][0, 0], nb)
    hs, rg_bufs, ffn_bufs = [], [], []
    for l in range(depth):
        g = w["norm_g"][l]
        if l < n_a:
            proj = _mm(xn, w["w_rg_in"][l], nb, 1024, F32)
            buf = rg_buf0[l]
            z, h_new = _rglru_step(proj, buf[:, 0], buf[:, 1], buf[:, 2], rg_h0[l],
                                   w["conv_rg_w"][l], w["conv_rg_b"][l], w["w_rg_a"][l],
                                   w["b_rg_a"][l], w["w_rg_i"][l], w["b_rg_i"][l],
                                   w["rg_lambda"][l])
            hs.append(h_new)
            rg_bufs.append(jnp.stack([buf[:, 1], buf[:, 2], proj[:, :proj.shape[1] // 2]], axis=1))
            x, xn = _mm_resnorm(z, w["w_rg_o"][l], x, g[1], g[2], nb)
        else:
            j = l - n_a
            cq = _mm_norm(xn, w["w_dq"][j], w["q_norm_g"][j], nb)
            ql, qr = _q_proj(cq, w["w_uq_ext"][j], w["w_uk_t"], cs, nb)
            o_lat = _attn_sample(page_table, ql, qr, cb, krb, cache_lat, cache_rope)
            o = _uv(o_lat, w["w_uv_h"])
            x, xn = _mm_resnorm(o, w["w_mla_o"][j], x, g[1], g[2], nb)
        fbuf = ffn_buf0[l]
        h, gate = _ffn_gu_step(xn, w["w_ffn_g"][l], w["w_ffn_u"][l], w["ffn_conv_w"][l],
                               w["ffn_conv_b"][l], fbuf[:, 0], fbuf[:, 1])
        ffn_bufs.append(jnp.stack([fbuf[:, 1], gate], axis=1))
        g_next = w["norm_g"][l + 1, 0] if l + 1 < depth else None
        x, xn = _mm_resnorm(h, w["w_ffn_d"][l], x, g[3], g_next, nb)
        if l == n_a - 1:
            c, kr, cb, krb = _kv(x, w["kv_in_g"], w["w_dkv_ext"], w["kv_norm_g"], cs, nb)
    return (x.reshape(nb, t, d), jnp.stack(hs), jnp.stack(rg_bufs), jnp.stack(ffn_bufs),
            c.reshape(nb, t, KV_LORA), kr.reshape(nb, t, ROPE_DIM))


def kernel(x_prompt, x_sample, state_rglru_h, state_rglru_conv, state_ffn_conv, cache_kv_latent, cache_k_rope, page_table, norm_g, w_rg_in, conv_rg_w, conv_rg_b, w_rg_a, b_rg_a, w_rg_i, b_rg_i, rg_lambda, w_rg_o, kv_in_g, w_dkv, kv_norm_g, w_uk, w_uv, w_dq, q_norm_g, w_uq, w_mla_o, w_ffn_g, w_ffn_u, ffn_conv_w, ffn_conv_b, w_ffn_d):
    w = _prep_weights({
        "norm_g": norm_g, "w_rg_in": w_rg_in, "conv_rg_w": conv_rg_w, "conv_rg_b": conv_rg_b,
        "w_rg_a": w_rg_a, "b_rg_a": b_rg_a, "w_rg_i": w_rg_i, "b_rg_i": b_rg_i,
        "rg_lambda": rg_lambda, "w_rg_o": w_rg_o, "kv_in_g": kv_in_g, "w_dkv": w_dkv,
        "kv_norm_g": kv_norm_g, "w_uk": w_uk, "w_uv": w_uv, "w_dq": w_dq, "q_norm_g": q_norm_g,
        "w_uq": w_uq, "w_mla_o": w_mla_o, "w_ffn_g": w_ffn_g, "w_ffn_u": w_ffn_u,
        "ffn_conv_w": ffn_conv_w, "ffn_conv_b": ffn_conv_b, "w_ffn_d": w_ffn_d})
    (y_p, p_h, p_rgb, p_ffb, p_c, p_kr) = _trunk_prompt(x_prompt, w)
    (y_s, s_h, s_rgb, s_ffb, s_c, s_kr) = _trunk_sample(
        x_sample, state_rglru_h, state_rglru_conv, state_ffn_conv, cache_kv_latent,
        cache_k_rope, page_table, w)
    return (y_p, y_s, p_h, s_h, p_rgb, s_rgb, p_ffb, s_ffb, p_c, s_c, p_kr, s_kr)
```
